```python
import math
import jax, jax.numpy as jnp
from jax import lax
import numpy as np

D_MODEL = 1024
BATCH = 4
SEQ = 4096
DEPTH = 2
DEC_BATCH = 128
DEC_SEQ = 4
PAST_LEN = 2048
PAGE_SIZE = 128

N_MIXERS = 2
N_ATTN = (DEPTH + 1) // 2
N_CONV = DEPTH // 2
N_HEADS = 16
HEAD_DIM = D_MODEL // N_HEADS
N_KV_HEADS = 4
GROUP = N_HEADS // N_KV_HEADS
ROT_DIM = HEAD_DIM // 4
ROPE_THETA = 500000.0
IDX_HEADS = 16
IDX_DIM = HEAD_DIM
TOPK_MAX = 256
Q_BLOCK = 128
CONV_W = 3
D_FF = 2816
ALPHA = (2 * DEPTH) ** 0.25
BETA = (8 * DEPTH) ** -0.25
LN_EPS = 1e-5
Q_W = N_HEADS * HEAD_DIM
KV_W = N_KV_HEADS * HEAD_DIM
QI_W = IDX_HEADS * IDX_DIM
ATTN_IN_W = Q_W + 2 * KV_W + QI_W + IDX_DIM + IDX_HEADS
ATTN_SPLITS = (Q_W, Q_W + KV_W, Q_W + 2 * KV_W, Q_W + 2 * KV_W + QI_W, Q_W + 2 * KV_W + QI_W + IDX_DIM)
IDX_SCALE = (IDX_HEADS ** -0.5) * (IDX_DIM ** -0.5)

kernel_name = "dsa_shortconv_deepnorm_step"


def layer_norm(x, g, b):
    x32 = x.astype(jnp.float32)
    mu = jnp.mean(x32, -1, keepdims=True)
    var = jnp.mean(jnp.square(x32 - mu), -1, keepdims=True)
    y = (x32 - mu) * lax.rsqrt(var + LN_EPS)
    return (y * g.astype(jnp.float32) + b.astype(jnp.float32)).astype(x.dtype)


def partial_rope(x, pos):
    half = ROT_DIM // 2
    inv = ROPE_THETA ** (-jnp.arange(half, dtype=jnp.float32) * (2.0 / ROT_DIM))
    ang = pos.astype(jnp.float32)[:, None] * inv[None, :]
    cos = jnp.cos(ang)[:, None, :].astype(x.dtype)
    sin = jnp.sin(ang)[:, None, :].astype(x.dtype)
    x1 = x[..., :half]
    x2 = x[..., half:ROT_DIM]
    return jnp.concatenate([x1 * cos - x2 * sin, x2 * cos + x1 * sin, x[..., ROT_DIM:]], -1)


def attn_projections(x, w_in, pos):
    B, T, _ = x.shape
    q, k, v, qi, ki, wi = jnp.split(x @ w_in, ATTN_SPLITS, axis=-1)
    q = partial_rope(q.reshape(B, T, N_HEADS, HEAD_DIM), pos)
    k = partial_rope(k.reshape(B, T, N_KV_HEADS, HEAD_DIM), pos)
    v = v.reshape(B, T, N_KV_HEADS, HEAD_DIM)
    qi = partial_rope(qi.reshape(B, T, IDX_HEADS, IDX_DIM), pos)
    ki = partial_rope(ki[:, :, None, :], pos)[:, :, 0, :]
    return q, k, v, qi, ki, wi * IDX_SCALE


def indexer_scores(qi, wi, ki):
    s = jnp.einsum('bqhd,bsd->bqhs', qi, ki, preferred_element_type=jnp.float32)
    return jnp.einsum('bqhs,bqh->bqs', jax.nn.relu(s), wi.astype(jnp.float32))


def sparse_attend(q, k_sel, v_sel, valid):
    B, Tq = q.shape[:2]
    qg = q.reshape(B, Tq, N_KV_HEADS, GROUP, HEAD_DIM)
    s = jnp.einsum('bqkgd,bqjkd->bqkgj', qg, k_sel, preferred_element_type=jnp.float32) * (HEAD_DIM ** -0.5)
    s = jnp.where(valid[:, :, None, None, :], s, -jnp.inf)
    p = jax.nn.softmax(s, axis=-1).astype(v_sel.dtype)
    o = jnp.einsum('bqkgj,bqjkd->bqkgd', p, v_sel)
    return o.reshape(B, Tq, Q_W)


def gather_rows(a, idx):
    return jax.vmap(lambda aa, ii: aa[ii])(a, idx)


def attn_prompt(x, w_in, w_out):
    B, T, _ = x.shape
    pos = jnp.arange(T)
    q, k, v, qi, ki, wi = attn_projections(x, w_in, pos)
    k_top = min(TOPK_MAX, T // 4)
    nb = T // Q_BLOCK

    def to_blocks(a):
        return jnp.moveaxis(a.reshape(B, nb, Q_BLOCK, *a.shape[2:]), 1, 0)

    def block(args):
        qb, qib, wib, start = args
        qpos = start + jnp.arange(Q_BLOCK)
        sc = indexer_scores(qib, wib, ki)
        sc = jnp.where((pos[None, :] <= qpos[:, None])[None], sc, -jnp.inf)
        _, idx = lax.top_k(sc, k_top)
        valid = idx <= qpos[None, :, None]
        return sparse_attend(qb, gather_rows(k, idx), gather_rows(v, idx), valid)

    starts = jnp.arange(nb) * Q_BLOCK
    o = lax.map(block, (to_blocks(q), to_blocks(qi), to_blocks(wi), starts))
    o = jnp.moveaxis(o, 0, 1).reshape(B, T, Q_W)
    return o @ w_out, k, v, ki


def attn_sample(x, w_in, w_out, cache_k, cache_v, cache_ik, page_table):
    Bd, Tn, _ = x.shape
    past = page_table.shape[1] * PAGE_SIZE
    pos = past + jnp.arange(Tn)
    q, k, v, qi, ki, wi = attn_projections(x, w_in, pos)
    L = past + Tn
    k_top = min(TOPK_MAX, L // 4)
    past_ik = cache_ik[page_table].reshape(Bd, past, IDX_DIM)
    all_ik = jnp.concatenate([past_ik, ki], axis=1)
    sc = indexer_scores(qi, wi, all_ik)
    sc = jnp.where((jnp.arange(L)[None, :] <= pos[:, None])[None], sc, -jnp.inf)
    _, idx = lax.top_k(sc, k_top)
    in_past = idx < past
    pidx = jnp.minimum(idx, past - 1)
    phys_page = gather_rows(page_table, pidx // PAGE_SIZE)
    flat_row = phys_page * PAGE_SIZE + pidx % PAGE_SIZE
    flat_k = cache_k.reshape(-1, N_KV_HEADS, HEAD_DIM)
    flat_v = cache_v.reshape(-1, N_KV_HEADS, HEAD_DIM)
    nidx = jnp.clip(idx - past, 0, Tn - 1)
    sel = in_past[..., None, None]
    k_sel = jnp.where(sel, flat_k[flat_row], gather_rows(k, nidx))
    v_sel = jnp.where(sel, flat_v[flat_row], gather_rows(v, nidx))
    valid = idx <= pos[None, :, None]
    o = sparse_attend(q, k_sel, v_sel, valid)
    return o @ w_out, k, v, ki


def causal_dwconv(z, prev, w):
    ext = jnp.concatenate([prev, z], axis=1)
    T = z.shape[1]
    y = ext[:, 0:T] * w[0]
    for j in range(1, CONV_W):
        y = y + ext[:, j:j + T] * w[j]
    return y, ext[:, -(CONV_W - 1):]


def conv_mixer(x, prev, w_in, conv_w, w_out):
    b, c, u = jnp.split(x @ w_in, 3, axis=-1)
    y, new_prev = causal_dwconv(c * u, prev, conv_w)
    return (b * y) @ w_out, new_prev


def conv_ffn(x, prev, w_up, conv_w, conv_b, w_down):
    a, g = jnp.split(x @ w_up, 2, axis=-1)
    ac, new_prev = causal_dwconv(a, prev, conv_w)
    h = jax.nn.silu(ac + conv_b) * g
    return h @ w_down, new_prev


def setup_inputs(seed: int = 0) -> dict:
    key = jax.random.key(seed)
    ks = jax.random.split(key, 24)
    n_pages = PAST_LEN // PAGE_SIZE
    used = DEC_BATCH * n_pages
    n_phys = used + max(1, used // 4)
    f32 = jnp.float32
    nrm = lambda k, s, sc: jax.random.normal(k, s, f32) * sc
    page_table = jax.random.permutation(ks[0], n_phys)[:used].reshape(DEC_BATCH, n_pages).astype(jnp.int32)
    return {
        "x_prompt": nrm(ks[1], (BATCH, SEQ, D_MODEL), 1.0),
        "x_sample": nrm(ks[2], (DEC_BATCH, DEC_SEQ, D_MODEL), 1.0),
        "cache_k": nrm(ks[3], (N_ATTN, n_phys, PAGE_SIZE, N_KV_HEADS, HEAD_DIM), 1.0),
        "cache_v": nrm(ks[4], (N_ATTN, n_phys, PAGE_SIZE, N_KV_HEADS, HEAD_DIM), 1.0),
        "cache_idx_k": nrm(ks[5], (N_ATTN, n_phys, PAGE_SIZE, IDX_DIM), 1.0),
        "state_conv": nrm(ks[6], (N_CONV, DEC_BATCH, CONV_W - 1, D_MODEL), 1.0),
        "state_ffn": nrm(ks[7], (DEPTH, DEC_BATCH, CONV_W - 1, D_FF), 1.0),
        "page_table": page_table,
        "w_attn_in": nrm(ks[8], (N_ATTN, D_MODEL, ATTN_IN_W), D_MODEL ** -0.5),
        "w_attn_out": nrm(ks[9], (N_ATTN, Q_W, D_MODEL), (Q_W ** -0.5) * BETA),
        "w_conv_in": nrm(ks[10], (N_CONV, D_MODEL, 3 * D_MODEL), D_MODEL ** -0.5),
        "conv_w": nrm(ks[11], (N_CONV, CONV_W, D_MODEL), CONV_W ** -0.5),
        "w_conv_out": nrm(ks[12], (N_CONV, D_MODEL, D_MODEL), (D_MODEL ** -0.5) * BETA),
        "w_ffn_up": nrm(ks[13], (DEPTH, D_MODEL, 2 * D_FF), D_MODEL ** -0.5),
        "ffn_conv_w": nrm(ks[14], (DEPTH, CONV_W, D_FF), CONV_W ** -0.5),
        "ffn_conv_b": nrm(ks[15], (DEPTH, D_FF), 0.02),
        "w_ffn_down": nrm(ks[16], (DEPTH, D_FF, D_MODEL), (D_FF ** -0.5) * BETA),
        "ln_g": 1.0 + nrm(ks[17], (DEPTH, 2, D_MODEL), 0.02),
        "ln_b": nrm(ks[18], (DEPTH, 2, D_MODEL), 0.02),
    }


def reference(x_prompt, x_sample, cache_k, cache_v, cache_idx_k, state_conv, state_ffn, page_table,
              w_attn_in, w_attn_out, w_conv_in, conv_w, w_conv_out, w_ffn_up, ffn_conv_w, ffn_conv_b,
              w_ffn_down, ln_g, ln_b):
    yp, ys = x_prompt, x_sample
    Bp = x_prompt.shape[0]
    kp_l, vp_l, ikp_l, ks_l, vs_l, iks_l = [], [], [], [], [], []
    cp_l, cs_l, fp_l, fs_l = [], [], [], []
    for i in range(DEPTH):
        j = i // N_MIXERS
        if i % N_MIXERS == 0:
            mp, kp, vp, ikp = attn_prompt(yp, w_attn_in[j], w_attn_out[j])
            ms, k_s, v_s, ik_s = attn_sample(ys, w_attn_in[j], w_attn_out[j], cache_k[j], cache_v[j],
                                             cache_idx_k[j], page_table)
            kp_l.append(kp); vp_l.append(vp); ikp_l.append(ikp)
            ks_l.append(k_s); vs_l.append(v_s); iks_l.append(ik_s)
        else:
            zeros_c = jnp.zeros((Bp, CONV_W - 1, D_MODEL), yp.dtype)
            mp, cp = conv_mixer(yp, zeros_c, w_conv_in[j], conv_w[j], w_conv_out[j])
            ms, cs = conv_mixer(ys, state_conv[j], w_conv_in[j], conv_w[j], w_conv_out[j])
            cp_l.append(cp); cs_l.append(cs)
        yp = layer_norm(ALPHA * yp + mp, ln_g[i, 0], ln_b[i, 0])
        ys = layer_norm(ALPHA * ys + ms, ln_g[i, 0], ln_b[i, 0])
        zeros_f = jnp.zeros((Bp, CONV_W - 1, D_FF), yp.dtype)
        fp, fsp = conv_ffn(yp, zeros_f, w_ffn_up[i], ffn_conv_w[i], ffn_conv_b[i], w_ffn_down[i])
        fs, fss = conv_ffn(ys, state_ffn[i], w_ffn_up[i], ffn_conv_w[i], ffn_conv_b[i], w_ffn_down[i])
        fp_l.append(fsp); fs_l.append(fss)
        yp = layer_norm(ALPHA * yp + fp, ln_g[i, 1], ln_b[i, 1])
        ys = layer_norm(ALPHA * ys + fs, ln_g[i, 1], ln_b[i, 1])
    return (yp, ys,
            jnp.stack(kp_l), jnp.stack(vp_l), jnp.stack(ikp_l),
            jnp.stack(ks_l), jnp.stack(vs_l), jnp.stack(iks_l),
            jnp.stack(cp_l), jnp.stack(cs_l),
            jnp.stack(fp_l), jnp.stack(fs_l))
```

```python
import functools

import jax
import jax.numpy as jnp
import numpy as np
from jax import lax
from jax.experimental import pallas as pl
from jax.experimental.pallas import tpu as pltpu

N_HEADS = 16
HEAD_DIM = 64
N_KV_HEADS = 4
GROUP = N_HEADS // N_KV_HEADS
ROT_DIM = HEAD_DIM // 4
ROT_HALF = ROT_DIM // 2
ROPE_THETA = 500000.0
IDX_HEADS = 16
IDX_DIM = 64
TOPK_MAX = 256
CONV_W = 3
LN_EPS = 1e-5
PAGE_SIZE = 128
Q_W = N_HEADS * HEAD_DIM
KV_W = N_KV_HEADS * HEAD_DIM
QI_W = IDX_HEADS * IDX_DIM
IDX_SCALE = (IDX_HEADS ** -0.5) * (IDX_DIM ** -0.5)
ATTN_SCALE = HEAD_DIM ** -0.5

LANES = 128
V7X_VMEM_LIMIT = 56 * 1024 * 1024
MASK_NEG = -1e30
INT_MIN = -(2 ** 31)
KEY_NEG_INF = int(np.int32(np.array(-np.inf, np.float32).view(np.int32)) ^ np.int32(0x7FFFFFFF))

BF16 = jnp.bfloat16
F32 = jnp.float32


def _dot(a, b):
    return jnp.dot(a, b, preferred_element_type=F32)


def _dot_nt(a, b):
    return lax.dot_general(a, b, (((1,), (1,)), ((), ())), preferred_element_type=F32)


def _sort_key(x):
    bits = lax.bitcast_convert_type(x, jnp.int32)
    return bits ^ ((bits >> 31) & jnp.int32(0x7FFFFFFF))


def _layer_norm(r, g, b):
    mu = jnp.mean(r, axis=-1, keepdims=True)
    d = r - mu
    var = jnp.mean(d * d, axis=-1, keepdims=True)
    return d * lax.rsqrt(var + LN_EPS) * g + b


def _rope(x, c, s_up, s_dn):
    outs = []
    for j in range(x.shape[1] // LANES):
        xc = x[:, j * LANES:(j + 1) * LANES]
        outs.append(xc * c + pltpu.roll(xc, LANES - ROT_HALF, 1) * s_up + pltpu.roll(xc, ROT_HALF, 1) * s_dn)
    return outs[0] if len(outs) == 1 else jnp.concatenate(outs, axis=1)


def _attn_proj_kernel(x_ref, wq_ref, wk_ref, wv_ref, wqi_ref, wt_ref, tab_ref, tabt_ref,
                      q_ref, k_ref, v_ref, kb_ref, vb_ref, qi_ref, tail_ref):
    xb = x_ref[...].astype(BF16)
    c, s_up, s_dn = tab_ref[0], tab_ref[1], tab_ref[2]
    q = _rope(_dot(xb, wq_ref[...]), c, s_up, s_dn) * ATTN_SCALE
    q_ref[...] = q.astype(BF16)
    k = _rope(_dot(xb, wk_ref[...]), c, s_up, s_dn)
    k_ref[...] = k
    kb_ref[...] = k.astype(BF16)
    v = _dot(xb, wv_ref[...])
    v_ref[...] = v
    vb_ref[...] = v.astype(BF16)
    qi_ref[...] = _rope(_dot(xb, wqi_ref[...]), c, s_up, s_dn).astype(BF16)
    tail_ref[...] = _rope(_dot(xb, wt_ref[...]), tabt_ref[0], tabt_ref[1], tabt_ref[2])


def _rope_tables(pos):
    inv = ROPE_THETA ** (-jnp.arange(ROT_HALF, dtype=F32) * (2.0 / ROT_DIM))
    ang = pos.astype(F32)[:, None] * inv[None, :]
    cos, sin = jnp.cos(ang), jnp.sin(ang)
    n = pos.shape[0]
    pad = jnp.zeros((n, HEAD_DIM - ROT_DIM), F32)
    c_head = jnp.concatenate([cos, cos, pad + 1.0], axis=1)
    up_head = jnp.concatenate([-sin, jnp.zeros_like(sin), pad], axis=1)
    dn_head = jnp.concatenate([jnp.zeros_like(sin), sin, pad], axis=1)
    heads = LANES // HEAD_DIM
    tab = jnp.stack([jnp.tile(c_head, (1, heads)), jnp.tile(up_head, (1, heads)), jnp.tile(dn_head, (1, heads))])
    zeros_head = jnp.zeros((n, HEAD_DIM), F32)
    wi_scale = jnp.concatenate([jnp.full((n, IDX_HEADS), IDX_SCALE, F32),
                                jnp.zeros((n, HEAD_DIM - IDX_HEADS), F32)], axis=1)
    tab_tail = jnp.stack([jnp.concatenate([c_head, wi_scale], axis=1),
                          jnp.concatenate([up_head, zeros_head], axis=1),
                          jnp.concatenate([dn_head, zeros_head], axis=1)])
    return tab, tab_tail


def _attn_proj(x, w_in, pos_tab, tm, tab_period):
    m, d = x.shape
    tab, tab_tail = pos_tab
    wb = w_in.astype(BF16)
    o0, o1, o2, o3 = Q_W, Q_W + KV_W, Q_W + 2 * KV_W, Q_W + 2 * KV_W + QI_W
    w_tail = jnp.pad(wb[:, o3:], ((0, 0), (0, LANES - (wb.shape[1] - o3))))
    full = lambda shape: pl.BlockSpec(shape, lambda i: (0,) * len(shape))
    row = lambda width: pl.BlockSpec((tm, width), lambda i: (i, 0))
    tab_spec = pl.BlockSpec((3, tm, LANES), lambda i: (0, i % tab_period, 0))
    return pl.pallas_call(
        _attn_proj_kernel,
        grid=(m // tm,),
        in_specs=[row(d), full((d, Q_W)), full((d, KV_W)), full((d, KV_W)), full((d, QI_W)), full((d, LANES)),
                  tab_spec, tab_spec],
        out_specs=[row(Q_W), row(KV_W), row(KV_W), row(KV_W), row(KV_W), row(QI_W), row(LANES)],
        out_shape=[jax.ShapeDtypeStruct((m, Q_W), BF16), jax.ShapeDtypeStruct((m, KV_W), F32),
                   jax.ShapeDtypeStruct((m, KV_W), F32), jax.ShapeDtypeStruct((m, KV_W), BF16),
                   jax.ShapeDtypeStruct((m, KV_W), BF16), jax.ShapeDtypeStruct((m, QI_W), BF16),
                   jax.ShapeDtypeStruct((m, LANES), F32)],
        compiler_params=pltpu.CompilerParams(dimension_semantics=("arbitrary",), vmem_limit_bytes=V7X_VMEM_LIMIT),
        name="attn_proj",
    )(x, wb[:, :o0], wb[:, o0:o1], wb[:, o1:o2], wb[:, o2:o3], w_tail, tab, tab_tail)


def _kth_largest_key(count_ge, k, rows):
    zero = jnp.zeros((rows, 1), jnp.int32)
    t = jnp.where(count_ge(zero) >= k, zero, jnp.int32(INT_MIN))

    def bit_body(bi, t):
        cand = t + jnp.left_shift(jnp.int32(1), 30 - bi)
        return jnp.where(count_ge(cand) >= k, cand, t)

    return lax.fori_loop(0, 31, bit_body, t)


def _attn_prompt_kernel(q_ref, qi_ref, tail_ref, kit_ref, kt_ref, vlo_ref, o_ref,
                        skey_ref, wb_ref, kipad_ref, kpad_ref, m_ref, l_ref, acc_ref, *, tq, kc, ktop):
    i = pl.program_id(1)
    nch = ((i + 1) * tq + kc - 1) // kc
    nl = kc // LANES

    kipad_ref[...] = jnp.zeros_like(kipad_ref)
    kpad_ref[...] = jnp.zeros_like(kpad_ref)
    for h in range(IDX_HEADS):
        wb_ref[h] = jnp.broadcast_to(tail_ref[:, IDX_DIM + h:IDX_DIM + h + 1], (tq, LANES))

    row = i * tq + lax.broadcasted_iota(jnp.int32, (tq, kc), 0)
    lane = lax.broadcasted_iota(jnp.int32, (tq, kc), 1)

    def idx_body(c, carry):
        kit = kit_ref[c]
        kipad_ref[0, 0:IDX_DIM, :] = kit
        kipad_ref[1, IDX_DIM:2 * IDX_DIM, :] = kit
        acc = [jnp.zeros((tq, LANES), F32) for _ in range(nl)]
        for pr in range(IDX_HEADS // 2):
            qp = qi_ref[:, pr * LANES:(pr + 1) * LANES]
            for e in range(2):
                h = 2 * pr + e
                s = _dot(qp, kipad_ref[e])
                wbh = wb_ref[h]
                for j in range(nl):
                    acc[j] = acc[j] + jnp.maximum(s[:, j * LANES:(j + 1) * LANES], 0.0) * wbh
        sc = jnp.concatenate(acc, axis=1)
        sc = jnp.where(c * kc + lane <= row, sc, -jnp.inf)
        skey_ref[c] = _sort_key(sc)
        return carry

    lax.fori_loop(0, nch, idx_body, 0)

    def count_ge(cand):
        def body(c, a):
            m = jnp.where(skey_ref[c] >= cand, 1.0, 0.0)
            for j in range(nl):
                a = a + m[:, j * LANES:(j + 1) * LANES]
            return a
        a = lax.fori_loop(0, nch, body, jnp.zeros((tq, LANES), F32))
        return jnp.sum(a, axis=1, keepdims=True)

    thr = _kth_largest_key(count_ge, float(ktop), tq)
    thr = jnp.maximum(thr, jnp.int32(KEY_NEG_INF + 1))

    m_ref[...] = jnp.full_like(m_ref, MASK_NEG)
    l_ref[...] = jnp.zeros_like(l_ref)
    acc_ref[...] = jnp.zeros_like(acc_ref)

    def att_body(c, carry):
        bias = jnp.where(skey_ref[c] >= thr, 0.0, MASK_NEG)
        for g in range(N_KV_HEADS):
            kt = kt_ref[c, g * HEAD_DIM:(g + 1) * HEAD_DIM, :]
            kpad_ref[g, 0, 0:HEAD_DIM, :] = kt
            kpad_ref[g, 1, HEAD_DIM:2 * HEAD_DIM, :] = kt
            vg = vlo_ref[c, g]
            for pr in range(GROUP // 2):
                pair = g * (GROUP // 2) + pr
                qp = q_ref[:, pair * LANES:(pair + 1) * LANES]
                for e in range(2):
                    h = 2 * pair + e
                    s = _dot(qp, kpad_ref[g, e]) + bias
                    m_old = m_ref[h]
                    m_new = jnp.maximum(m_old, jnp.max(s, axis=1, keepdims=True))
                    alpha = jnp.exp(m_old - m_new)
                    p = jnp.exp(s - m_new)
                    l_ref[h] = alpha * l_ref[h] + jnp.sum(p, axis=1, keepdims=True)
                    m_ref[h] = m_new
                    acc_ref[h] = alpha * acc_ref[h] + _dot(p.astype(BF16), vg)
        return carry

    lax.fori_loop(0, nch, att_body, 0)

    for pair in range(N_HEADS // 2):
        he, ho = 2 * pair, 2 * pair + 1
        oe = acc_ref[he] / l_ref[he]
        oo = acc_ref[ho] / l_ref[ho]
        o_ref[:, pair * LANES:(pair + 1) * LANES] = (oe + pltpu.roll(oo, HEAD_DIM, 1)).astype(BF16)


def _attn_prompt(q, qi, tail, kb, vb, batch, seq, tq, kc):
    ktop = min(TOPK_MAX, seq // 4)
    nq = seq // tq
    nc = seq // kc
    kit = tail[:, :IDX_DIM].astype(BF16).reshape(batch, nc, kc, IDX_DIM).transpose(0, 1, 3, 2)
    kt = kb.reshape(batch, nc, kc, KV_W).transpose(0, 1, 3, 2)
    vlo = vb.reshape(batch, nc, kc, N_KV_HEADS, HEAD_DIM).transpose(0, 1, 3, 2, 4)
    vlo = jnp.pad(vlo, ((0, 0),) * 4 + ((0, LANES - HEAD_DIM),))
    rows = lambda width: pl.BlockSpec((tq, width), lambda b, i: (b * nq + i, 0))
    kernel = functools.partial(_attn_prompt_kernel, tq=tq, kc=kc, ktop=ktop)
    return pl.pallas_call(
        kernel,
        grid=(batch, nq),
        in_specs=[rows(Q_W), rows(QI_W), rows(LANES),
                  pl.BlockSpec((None, nc, IDX_DIM, kc), lambda b, i: (b, 0, 0, 0)),
                  pl.BlockSpec((None, nc, KV_W, kc), lambda b, i: (b, 0, 0, 0)),
                  pl.BlockSpec((None, nc, N_KV_HEADS, kc, LANES), lambda b, i: (b, 0, 0, 0, 0))],
        out_specs=rows(Q_W),
        out_shape=jax.ShapeDtypeStruct((batch * seq, Q_W), BF16),
        scratch_shapes=[pltpu.VMEM((nc, tq, kc), jnp.int32),
                        pltpu.VMEM((IDX_HEADS, tq, LANES), F32),
                        pltpu.VMEM((2, 2 * IDX_DIM, kc), BF16),
                        pltpu.VMEM((N_KV_HEADS, 2, 2 * HEAD_DIM, kc), BF16),
                        pltpu.VMEM((N_HEADS, tq, 1), F32),
                        pltpu.VMEM((N_HEADS, tq, 1), F32),
                        pltpu.VMEM((N_HEADS, tq, LANES), F32)],
        compiler_params=pltpu.CompilerParams(dimension_semantics=("arbitrary", "arbitrary"),
                                             vmem_limit_bytes=V7X_VMEM_LIMIT),
        name="attn_prompt",
    )(q, qi, tail, kit, kt, vlo)


SAMPLE_ROWS = 8


def _attn_sample_kernel(pt_ref, qbd_ref, qis_ref, wrow_ref, *refs, n_pages, tn, ktop):
    npg = n_pages + 1
    ik_refs = refs[0:npg]
    k_refs = refs[npg:2 * npg]
    v_refs = refs[2 * npg:3 * npg]
    o_ref = refs[3 * npg]
    skey_ref, satt_ref = refs[3 * npg + 1:]
    rows = IDX_HEADS * SAMPLE_ROWS
    del pt_ref

    qis = qis_ref[...]
    wrow = wrow_ref[...]
    for p in range(npg):
        s = _dot_nt(qis, ik_refs[p][...].astype(BF16))
        r = jnp.maximum(s, 0.0) * wrow
        sc = r[0:SAMPLE_ROWS]
        for h in range(1, IDX_HEADS):
            sc = sc + r[h * SAMPLE_ROWS:(h + 1) * SAMPLE_ROWS]
        if p == n_pages:
            t_new = lax.broadcasted_iota(jnp.int32, sc.shape, 1)
            t_q = lax.broadcasted_iota(jnp.int32, sc.shape, 0)
            sc = jnp.where((t_new <= t_q) & (t_new < tn), sc, -jnp.inf)
        skey_ref[:, p * PAGE_SIZE:(p + 1) * PAGE_SIZE] = _sort_key(sc)

    def count_ge(cand):
        m = jnp.where(skey_ref[...] >= cand, 1.0, 0.0)
        return jnp.sum(m, axis=1, keepdims=True)

    thr = _kth_largest_key(count_ge, float(ktop), SAMPLE_ROWS)
    thr = jnp.maximum(thr, jnp.int32(KEY_NEG_INF + 1))
    bias = jnp.where(skey_ref[...] >= thr, 0.0, MASK_NEG)

    qbd = qbd_ref[...]
    for p in range(npg):
        bias_p = bias[:, p * PAGE_SIZE:(p + 1) * PAGE_SIZE]
        bias_rows = jnp.concatenate([bias_p] * IDX_HEADS, axis=0)
        satt_ref[:, p * PAGE_SIZE:(p + 1) * PAGE_SIZE] = _dot_nt(qbd, k_refs[p][...].astype(BF16)) + bias_rows
    s_all = satt_ref[...]
    m = jnp.max(s_all, axis=1, keepdims=True)
    pr = jnp.exp(s_all - m)
    l = jnp.sum(pr, axis=1, keepdims=True)
    prb = pr.astype(BF16)
    acc = jnp.zeros((rows, KV_W), F32)
    for p in range(npg):
        acc = acc + _dot(prb[:, p * PAGE_SIZE:(p + 1) * PAGE_SIZE], v_refs[p][...].astype(BF16))
    o_ref[...] = acc / l


def _attn_sample(q, qi, tail, k_new, v_new, cache_k, cache_v, cache_ik, layer, page_table, tn):
    bd = q.shape[0]
    n_pages = page_table.shape[1]
    past = n_pages * PAGE_SIZE
    ktop = min(TOPK_MAX, (past + tn) // 4)
    rows = IDX_HEADS * SAMPLE_ROWS
    n_phys = cache_k.shape[1]

    def head_rows(a):
        a = jnp.pad(a.transpose(0, 2, 1, 3), ((0, 0), (0, 0), (0, SAMPLE_ROWS - tn), (0, 0)))
        return a.reshape(bd, rows, a.shape[-1])

    qh = head_rows(q.reshape(bd, tn, N_HEADS, HEAD_DIM))
    sel = (jnp.arange(N_HEADS)[:, None] // GROUP == jnp.arange(N_KV_HEADS)[None, :]).astype(BF16)
    qbd = (qh.reshape(bd, N_HEADS, SAMPLE_ROWS, 1, HEAD_DIM) * sel[None, :, None, :, None]).reshape(bd, rows, KV_W)
    qis = head_rows(qi.reshape(bd, tn, IDX_HEADS, IDX_DIM))
    wi = tail[:, :, IDX_DIM:IDX_DIM + IDX_HEADS]
    wrow = jnp.broadcast_to(head_rows(wi[..., None]), (bd, rows, PAGE_SIZE))
    pad_rows = lambda a: jnp.pad(a, ((0, 0), (0, PAGE_SIZE - tn), (0, 0)))
    ik_new = pad_rows(tail[:, :, :IDX_DIM])
    k_newp, v_newp = pad_rows(k_new), pad_rows(v_new)
    ck = cache_k.reshape(cache_k.shape[0], n_phys, PAGE_SIZE, KV_W)
    cv = cache_v.reshape(cache_v.shape[0], n_phys, PAGE_SIZE, KV_W)
    pt = page_table.reshape(-1)

    def page_spec(width, p):
        return pl.BlockSpec((None, None, PAGE_SIZE, width), lambda b, pt_ref, p=p: (layer, pt_ref[b * n_pages + p], 0, 0))

    per_b = lambda r, width: pl.BlockSpec((None, r, width), lambda b, pt_ref: (b, 0, 0))
    in_specs = [per_b(rows, KV_W), per_b(rows, IDX_DIM), per_b(rows, PAGE_SIZE)]
    in_specs += [page_spec(IDX_DIM, p) for p in range(n_pages)] + [per_b(PAGE_SIZE, IDX_DIM)]
    in_specs += [page_spec(KV_W, p) for p in range(n_pages)] + [per_b(PAGE_SIZE, KV_W)]
    in_specs += [page_spec(KV_W, p) for p in range(n_pages)] + [per_b(PAGE_SIZE, KV_W)]
    npg = n_pages + 1
    kernel = functools.partial(_attn_sample_kernel, n_pages=n_pages, tn=tn, ktop=ktop)
    ofull = pl.pallas_call(
        kernel,
        grid_spec=pltpu.PrefetchScalarGridSpec(
            num_scalar_prefetch=1, grid=(bd,), in_specs=in_specs, out_specs=per_b(rows, KV_W),
            scratch_shapes=[pltpu.VMEM((SAMPLE_ROWS, npg * PAGE_SIZE), jnp.int32),
                            pltpu.VMEM((rows, npg * PAGE_SIZE), F32)]),
        out_shape=jax.ShapeDtypeStruct((bd, rows, KV_W), F32),
        compiler_params=pltpu.CompilerParams(dimension_semantics=("arbitrary",), vmem_limit_bytes=V7X_VMEM_LIMIT),
        name="attn_sample",
    )(pt, qbd, qis, wrow, *([cache_ik] * n_pages), ik_new, *([ck] * n_pages), k_newp, *([cv] * n_pages), v_newp)
    o = ofull.reshape(bd, N_KV_HEADS, GROUP, SAMPLE_ROWS, N_KV_HEADS, HEAD_DIM)[:, :, :, :tn]
    o = jnp.stack([o[:, g, :, :, g, :] for g in range(N_KV_HEADS)], axis=1)
    return o.transpose(0, 3, 1, 2, 4).reshape(bd, tn, Q_W)


def _proj_ln_kernel(x_ref, a_ref, w_ref, g_ref, b_ref, y_ref, *, alpha):
    r = alpha * x_ref[...] + _dot(a_ref[...], w_ref[...])
    y_ref[...] = _layer_norm(r, g_ref[...], b_ref[...])


def _proj_ln(x, a, w, g, b, alpha, tm):
    m, d = x.shape
    kdim = a.shape[1]
    row = lambda width: pl.BlockSpec((tm, width), lambda i: (i, 0))
    full = lambda shape: pl.BlockSpec(shape, lambda i: (0,) * len(shape))
    return pl.pallas_call(
        functools.partial(_proj_ln_kernel, alpha=alpha),
        grid=(m // tm,),
        in_specs=[row(d), row(kdim), full((kdim, d)), full((1, d)), full((1, d))],
        out_specs=row(d),
        out_shape=jax.ShapeDtypeStruct((m, d), F32),
        compiler_params=pltpu.CompilerParams(dimension_semantics=("arbitrary",), vmem_limit_bytes=V7X_VMEM_LIMIT),
        name="proj_ln",
    )(x, a.astype(BF16), w.astype(BF16), g.reshape(1, d), b.reshape(1, d))


def _causal_conv(ext_ref, z, cw_ref, cols, tm, halo, step):
    ext_ref[halo:halo + tm, cols] = z
    y = ext_ref[halo - 2 * step:halo - 2 * step + tm, cols] * cw_ref[0:1, cols]
    y = y + ext_ref[halo - step:halo - step + tm, cols] * cw_ref[1:2, cols]
    return y + z * cw_ref[2:3, cols]


def _conv_carry(ext_ref, st_ref, tm, halo, step):
    last = ext_ref[tm + halo - 2 * step:tm + halo, :]
    st_ref[...] = last
    ext_ref[halo - 2 * step:halo, :] = last


def _conv_ffn_kernel(x_ref, init_ref, wup_ref, cw_ref, cb_ref, wdn_ref, g_ref, b_ref, y_ref, st_ref, ext_ref,
                     *, tm, halo, step, fc, alpha):
    f = cw_ref.shape[1]

    @pl.when(pl.program_id(1) == 0)
    def _():
        ext_ref[0:halo, :] = init_ref[...]

    x = x_ref[...]
    xb = x.astype(BF16)
    acc = jnp.zeros(x.shape, F32)
    for c in range(f // fc):
        cols = slice(c * fc, (c + 1) * fc)
        a = _dot(xb, wup_ref[:, cols])
        gate = _dot(xb, wup_ref[:, f + c * fc:f + (c + 1) * fc])
        ac = _causal_conv(ext_ref, a, cw_ref, cols, tm, halo, step) + cb_ref[:, cols]
        h = ac / (1.0 + jnp.exp(-ac)) * gate
        acc = acc + _dot(h.astype(BF16), wdn_ref[cols, :])
    _conv_carry(ext_ref, st_ref, tm, halo, step)
    y_ref[...] = _layer_norm(alpha * x + acc, g_ref[...], b_ref[...])


def _conv_mixer_kernel(x_ref, init_ref, win_ref, cw_ref, wout_ref, g_ref, b_ref, y_ref, st_ref, ext_ref,
                       *, tm, halo, step, fc, alpha):
    d = cw_ref.shape[1]

    @pl.when(pl.program_id(1) == 0)
    def _():
        ext_ref[0:halo, :] = init_ref[...]

    x = x_ref[...]
    xb = x.astype(BF16)
    acc = jnp.zeros(x.shape, F32)
    for c in range(d // fc):
        cols = slice(c * fc, (c + 1) * fc)
        bb = _dot(xb, win_ref[:, cols])
        cc = _dot(xb, win_ref[:, d + c * fc:d + (c + 1) * fc])
        uu = _dot(xb, win_ref[:, 2 * d + c * fc:2 * d + (c + 1) * fc])
        y = _causal_conv(ext_ref, cc * uu, cw_ref, cols, tm, halo, step)
        acc = acc + _dot((bb * y).astype(BF16), wout_ref[cols, :])
    _conv_carry(ext_ref, st_ref, tm, halo, step)
    y_ref[...] = _layer_norm(alpha * x + acc, g_ref[...], b_ref[...])


def _conv_block(kernel_fn, name, x, init, weights, cw, g, b, alpha, tm, halo, step, fc):
    bsz, t, d = x.shape
    c = cw.shape[1]
    pre, post = weights
    full = lambda a: pl.BlockSpec(a.shape, lambda bi, j: (0,) * a.ndim, pipeline_mode=pl.Buffered(1))
    seq = pl.BlockSpec((None, tm, d), lambda bi, j: (bi, j, 0))
    in_specs = ([seq, pl.BlockSpec((None, halo, c), lambda bi, j: (bi, 0, 0))] + [full(a) for a in pre] + [full(cw)]
                + [full(a) for a in post] + [pl.BlockSpec((1, d), lambda bi, j: (0, 0))] * 2)
    kernel = functools.partial(kernel_fn, tm=tm, halo=halo, step=step, fc=fc, alpha=alpha)
    return pl.pallas_call(
        kernel,
        grid=(bsz, t // tm),
        in_specs=in_specs,
        out_specs=[seq, pl.BlockSpec((None, 2 * step, c), lambda bi, j: (bi, 0, 0))],
        out_shape=[jax.ShapeDtypeStruct((bsz, t, d), F32), jax.ShapeDtypeStruct((bsz, 2 * step, c), F32)],
        scratch_shapes=[pltpu.VMEM((tm + halo, c), F32)],
        compiler_params=pltpu.CompilerParams(dimension_semantics=("arbitrary", "arbitrary"),
                                             vmem_limit_bytes=V7X_VMEM_LIMIT),
        name=name,
    )(x, init, *pre, cw, *post, g.reshape(1, d), b.reshape(1, d))


PROMPT_TM = 512
PROMPT_TQ = 128
PROMPT_KC = 512
CONV_FC = 256
PROMPT_HALO = 8


def kernel(x_prompt, x_sample, cache_k, cache_v, cache_idx_k, state_conv, state_ffn, page_table,
           w_attn_in, w_attn_out, w_conv_in, conv_w, w_conv_out, w_ffn_up, ffn_conv_w, ffn_conv_b,
           w_ffn_down, ln_g, ln_b):
    bp, t, d = x_prompt.shape
    bd, tn, _ = x_sample.shape
    depth = ln_g.shape[0]
    alpha = (2 * depth) ** 0.25
    past = page_table.shape[1] * PAGE_SIZE
    d_ff = ffn_conv_w.shape[-1]
    tm = min(PROMPT_TM, t)
    ms = bd * tn

    yp = x_prompt
    ys = x_sample.transpose(1, 0, 2).reshape(1, ms, d)
    tabs_p = _rope_tables(jnp.arange(t))
    tabs_s = _rope_tables(jnp.repeat(past + jnp.arange(tn), bd))
    halo_s, step_s = 2 * bd, bd

    def sample_state(st):
        return st.transpose(1, 0, 2).reshape(1, 2 * bd, st.shape[-1])

    def sample_state_out(st):
        return st.reshape(2, bd, st.shape[-1]).transpose(1, 0, 2)

    outs = {k: [] for k in ("kp", "vp", "ikp", "ks", "vs", "iks", "cp", "cs", "fp", "fs")}
    for i in range(depth):
        j = i // 2
        g0, b0, g1, b1 = ln_g[i, 0], ln_b[i, 0], ln_g[i, 1], ln_b[i, 1]
        if i % 2 == 0:
            xp2 = yp.reshape(bp * t, d)
            q, k, v, kb, vb, qi, tail = _attn_proj(xp2, w_attn_in[j], tabs_p, tm, t // tm)
            o = _attn_prompt(q, qi, tail, kb, vb, bp, t, min(PROMPT_TQ, t), min(PROMPT_KC, t))
            yp = _proj_ln(xp2, o, w_attn_out[j], g0, b0, alpha, tm).reshape(bp, t, d)
            outs["kp"].append(k.reshape(bp, t, N_KV_HEADS, HEAD_DIM))
            outs["vp"].append(v.reshape(bp, t, N_KV_HEADS, HEAD_DIM))
            outs["ikp"].append(tail[:, :IDX_DIM].reshape(bp, t, IDX_DIM))

            xs2 = ys.reshape(ms, d)
            q, k, v, _, _, qi, tail = _attn_proj(xs2, w_attn_in[j], tabs_s, ms, 1)
            bm = lambda a: a.reshape(tn, bd, a.shape[-1]).transpose(1, 0, 2)
            k, v, tail = bm(k), bm(v), bm(tail)
            o = _attn_sample(bm(q), bm(qi), tail, k, v, cache_k, cache_v, cache_idx_k, j, page_table, tn)
            o = o.transpose(1, 0, 2).reshape(ms, Q_W)
            ys = _proj_ln(xs2, o, w_attn_out[j], g0, b0, alpha, ms).reshape(1, ms, d)
            outs["ks"].append(k.reshape(bd, tn, N_KV_HEADS, HEAD_DIM))
            outs["vs"].append(v.reshape(bd, tn, N_KV_HEADS, HEAD_DIM))
            outs["iks"].append(tail[:, :, :IDX_DIM])
        else:
            wts = ([w_conv_in[j].astype(BF16)], [w_conv_out[j].astype(BF16)])
            yp, cp = _conv_block(_conv_mixer_kernel, "conv_mixer", yp, jnp.zeros((bp, PROMPT_HALO, d), F32), wts,
                                 conv_w[j], g0, b0, alpha, tm, PROMPT_HALO, 1, CONV_FC)
            ys, cs = _conv_block(_conv_mixer_kernel, "conv_mixer_s", ys, sample_state(state_conv[j]), wts,
                                 conv_w[j], g0, b0, alpha, ms, halo_s, step_s, CONV_FC)
            outs["cp"].append(cp)
            outs["cs"].append(sample_state_out(cs))
        wts = ([w_ffn_up[i].astype(BF16)], [ffn_conv_b[i].reshape(1, d_ff), w_ffn_down[i].astype(BF16)])
        yp, fp = _conv_block(_conv_ffn_kernel, "conv_ffn", yp, jnp.zeros((bp, PROMPT_HALO, d_ff), F32), wts,
                             ffn_conv_w[i], g1, b1, alpha, tm, PROMPT_HALO, 1, CONV_FC)
        ys, fs = _conv_block(_conv_ffn_kernel, "conv_ffn_s", ys, sample_state(state_ffn[i]), wts,
                             ffn_conv_w[i], g1, b1, alpha, ms, halo_s, step_s, CONV_FC)
        outs["fp"].append(fp)
        outs["fs"].append(sample_state_out(fs))

    ys_out = ys.reshape(tn, bd, d).transpose(1, 0, 2)
    st = lambda name: jnp.stack(outs[name])
    return (yp, ys_out, st("kp"), st("vp"), st("ikp"), st("ks"), st("vs"), st("iks"),
            st("cp"), st("cs"), st("fp"), st("fs"))
```
